```python
import jax, jax.numpy as jnp
from jax import lax
import numpy as np

D_MODEL = 2048
BATCH = 32
SEQ = 256
DEPTH = 2
DEC_BATCH = 4
DEC_SEQ = 1024
PAST_LEN = 256

GRID_W = 64
HEAD_DIM = 128
H_A = 4
H_B = 8
KV_B = 2
G_B = H_B // KV_B
H_C = 4
NA_ROWS = 8
NA_COLS = 16
NA_QCOLS = 16
NA_UCOLS = NA_QCOLS + NA_COLS
Q_BLOCK = 128
RET_CHUNK = 64
ROPE_THETA = 10000.0
FFN_DIM = 5504
CONV_WIDTH = 3
RMS_EPS = 1e-6
NEG_INF = -1e30
PROJ_SIZES = (H_A * HEAD_DIM, H_A * HEAD_DIM, H_A * HEAD_DIM,
              H_B * HEAD_DIM, KV_B * HEAD_DIM, KV_B * HEAD_DIM,
              H_C * HEAD_DIM, H_C * HEAD_DIM, H_C * HEAD_DIM, H_C * HEAD_DIM)
IN_DIM = 3 * H_A * HEAD_DIM + (H_B + 2 * KV_B) * HEAD_DIM + 4 * H_C * HEAD_DIM
MIX_DIM = (H_A + H_B + H_C) * HEAD_DIM

kernel_name = 'hybrid_natten_gqa_retention_dit_step'

f32 = jnp.float32


def rms_norm(x, g):
    xf = x.astype(f32)
    y = xf * lax.rsqrt(jnp.mean(xf * xf, axis=-1, keepdims=True) + RMS_EPS)
    return (y * g.astype(f32)).astype(x.dtype)


def adaln(cond, w, b):
    mod = jax.nn.silu(cond) @ w + b
    return jnp.split(mod[:, None, :], 6, axis=-1)


def modulate(h, shift, scale):
    return h * (1.0 + scale) + shift


def in_projection(h, w_in):
    p = h @ w_in
    parts = []
    off = 0
    for size in PROJ_SIZES:
        parts.append(p[..., off:off + size])
        off += size
    return parts


def head_inputs(h, w_in, qn_a, kn_a, qn_b, kn_b):
    B, T, _ = h.shape
    qa, ka, va, qb, kb, vb, qc, kc, vc, gc = in_projection(h, w_in)
    heads = lambda t, n: t.reshape(B, T, n, HEAD_DIM)
    qa = rms_norm(heads(qa, H_A), qn_a)
    ka = rms_norm(heads(ka, H_A), kn_a)
    va = heads(va, H_A)
    qb = rms_norm(heads(qb, H_B), qn_b)
    kb = rms_norm(heads(kb, KV_B), kn_b)
    vb = heads(vb, KV_B)
    return qa, ka, va, qb, kb, vb, heads(qc, H_C), heads(kc, H_C), heads(vc, H_C), gc


def axial_rope_tables(T):
    t = jnp.arange(T)
    nf = HEAD_DIM // 4
    inv_freq = jnp.power(ROPE_THETA, -jnp.arange(nf, dtype=f32) / nf)
    rows = (t // GRID_W).astype(f32)
    cols = (t % GRID_W).astype(f32)
    ang = jnp.stack([rows[:, None] * inv_freq, cols[:, None] * inv_freq], axis=1)
    return jnp.cos(ang), jnp.sin(ang)


def apply_axial_rope(x, cos, sin):
    B, T, H, d = x.shape
    xr = x.astype(f32).reshape(B, T, H, 2, 2, d // 4)
    x1, x2 = xr[..., 0, :], xr[..., 1, :]
    c = cos[None, :, None]
    s = sin[None, :, None]
    out = jnp.stack([x1 * c - x2 * s, x1 * s + x2 * c], axis=-2)
    return out.reshape(B, T, H, d).astype(x.dtype)


def block_attention(q, k, v):
    B, Tq, Hkv, G, d = q.shape
    nb = Tq // Q_BLOCK
    qb = q.reshape(B, nb, Q_BLOCK, Hkv, G, d).transpose(1, 0, 2, 3, 4, 5)
    scale = d ** -0.5

    def one_block(qblk):
        s = jnp.einsum('bqhgd,bkhd->bhgqk', qblk, k, preferred_element_type=f32) * scale
        p = jax.nn.softmax(s, axis=-1).astype(v.dtype)
        return jnp.einsum('bhgqk,bkhd->bqhgd', p, v)

    o = lax.map(one_block, qb)
    return o.transpose(1, 0, 2, 3, 4, 5).reshape(B, Tq, Hkv, G, d)


def neighbourhood_attention(q, k, v, k_ctx, v_ctx, rel_bias):
    B, T, H, d = q.shape
    rows = T // GRID_W
    kr = min(NA_ROWS, rows)
    nb = GRID_W // NA_QCOLS
    r = jnp.arange(rows)
    row_idx = jnp.clip(r - kr // 2, 0, rows - kr)[:, None] + jnp.arange(kr)[None, :]
    blk = jnp.arange(nb)
    col_idx = (jnp.clip(blk * NA_QCOLS - NA_COLS // 2, 0, GRID_W - NA_UCOLS)[:, None]
               + jnp.arange(NA_UCOLS)[None, :])
    q_col = blk[:, None] * NA_QCOLS + jnp.arange(NA_QCOLS)[None, :]
    c_start = jnp.clip(q_col - NA_COLS // 2, 0, GRID_W - NA_COLS)[..., None]
    in_win = (col_idx[:, None, :] >= c_start) & (col_idx[:, None, :] < c_start + NA_COLS)
    dr = row_idx - r[:, None] + (NA_ROWS - 1)
    dc = jnp.clip(col_idx[:, None, :] - q_col[..., None] + (NA_COLS - 1), 0, 2 * NA_COLS - 2)
    bias = rel_bias[:, dr[:, :, None, None, None], dc[None, None]].astype(f32)
    bias = jnp.where(in_win[None, None, None], bias, NEG_INF).transpose(0, 1, 3, 4, 2, 5)
    kg = k.reshape(B, rows, GRID_W, H, d)[:, row_idx][:, :, :, col_idx]
    vg = v.reshape(B, rows, GRID_W, H, d)[:, row_idx][:, :, :, col_idx]
    qg = q.reshape(B, rows, nb, NA_QCOLS, H, d)
    scale = d ** -0.5
    s_loc = jnp.einsum('brnqhd,brknuhd->bhrnqku', qg, kg, preferred_element_type=f32) * scale + bias[None]
    n_loc = kr * NA_UCOLS
    s_loc = s_loc.reshape(B, H, rows, nb, NA_QCOLS, n_loc)
    s_ctx = jnp.einsum('brnqhd,bshd->bhrnqs', qg, k_ctx, preferred_element_type=f32) * scale
    p = jax.nn.softmax(jnp.concatenate([s_loc, s_ctx], axis=-1), axis=-1).astype(v.dtype)
    p_loc = p[..., :n_loc].reshape(B, H, rows, nb, NA_QCOLS, kr, NA_UCOLS)
    p_ctx = p[..., n_loc:]
    o = (jnp.einsum('bhrnqku,brknuhd->brnqhd', p_loc, vg)
         + jnp.einsum('bhrnqs,bshd->brnqhd', p_ctx, v_ctx))
    return o.reshape(B, T, H, d)


def retention_scan(q, k, v, log_gamma, state0):
    B, H, T, d = q.shape
    n = T // RET_CHUNK
    qc = q.reshape(B, H, n, RET_CHUNK, d)
    kc = k.reshape(B, H, n, RET_CHUNK, d)
    vc = v.reshape(B, H, n, RET_CHUNK, d)
    pos = jnp.arange(RET_CHUNK, dtype=f32)
    diff = pos[:, None] - pos[None, :]
    intra = jnp.where(diff >= 0, jnp.exp(log_gamma[:, None, None] * jnp.maximum(diff, 0.0)), 0.0)
    scores = jnp.einsum('bhnid,bhnjd->bhnij', qc, kc) * intra[None, :, None]
    o_inner = jnp.einsum('bhnij,bhnje->bhnie', scores, vc)
    k_w = jnp.exp(log_gamma[:, None] * (RET_CHUNK - 1.0 - pos))
    kv = jnp.einsum('bhnjd,bhnje->nbhde', kc * k_w[None, :, None, :, None], vc)
    chunk_decay = jnp.exp(log_gamma * RET_CHUNK)[None, :, None, None]

    def step(R, kv_n):
        return R * chunk_decay + kv_n, R

    R_final, R_prev = lax.scan(step, state0.astype(f32), kv)
    q_w = jnp.exp(log_gamma[:, None] * (pos + 1.0))
    o_cross = jnp.einsum('bhnid,nbhde->bhnie', qc * q_w[None, :, None, :, None], R_prev)
    return (o_inner + o_cross).reshape(B, H, T, d), R_final


def retention_mixer(qc, kc, vc, gc, decay_logit, norm_g, state_f, state_b):
    B, T = qc.shape[:2]
    to_bhtd = lambda t: t.astype(f32).transpose(0, 2, 1, 3)
    q = to_bhtd(qc)
    k = to_bhtd(kc) * (HEAD_DIM ** -0.5)
    v = to_bhtd(vc)
    log_gamma = jax.nn.log_sigmoid(decay_logit.astype(f32))
    flip = lambda t: jnp.flip(t, axis=2)
    o_f, s_f = retention_scan(q, k, v, log_gamma[0], state_f)
    o_b, s_b = retention_scan(flip(q), flip(k), flip(v), log_gamma[1], state_b)
    o = (o_f + flip(o_b)).transpose(0, 2, 1, 3)
    o = rms_norm(o, norm_g.reshape(H_C, HEAD_DIM)).reshape(B, T, H_C * HEAD_DIM)
    return o.astype(gc.dtype) * jax.nn.silu(gc), s_f, s_b


def conv_ffn(h, w_up, conv_w, conv_b, w_down):
    u = h @ w_up
    u = lax.conv_general_dilated(u, conv_w[:, None, :], window_strides=(1,),
                                 padding=((CONV_WIDTH // 2, CONV_WIDTH // 2),),
                                 dimension_numbers=('NWC', 'WIO', 'NWC'),
                                 feature_group_count=u.shape[-1]) + conv_b
    gate, val = jnp.split(u, 2, axis=-1)
    return (jax.nn.silu(gate) * val) @ w_down


def setup_inputs(seed: int = 0) -> dict:
    key = jax.random.key(seed)
    ks = jax.random.split(key, 32)
    nrm = lambda i, shape, s: jax.random.normal(ks[i], shape, f32) * s
    base_logit = jnp.log(jnp.exp2(5.0 + jnp.arange(H_C, dtype=f32)) - 1.0)
    return {
        'x_prompt': nrm(0, (BATCH, SEQ, D_MODEL), 1.0),
        'x_sample': nrm(1, (DEC_BATCH, DEC_SEQ, D_MODEL), 1.0),
        'cache_ka': nrm(2, (DEC_BATCH, DEPTH, PAST_LEN, H_A, HEAD_DIM), 1.0),
        'cache_va': nrm(3, (DEC_BATCH, DEPTH, PAST_LEN, H_A, HEAD_DIM), 1.0),
        'cache_kb': nrm(4, (DEC_BATCH, DEPTH, PAST_LEN, KV_B, HEAD_DIM), 1.0),
        'cache_vb': nrm(5, (DEC_BATCH, DEPTH, PAST_LEN, KV_B, HEAD_DIM), 1.0),
        'state_ret': nrm(6, (DEC_BATCH, DEPTH, 2, H_C, HEAD_DIM, HEAD_DIM), 0.5),
        'c': nrm(7, (DEC_BATCH, D_MODEL), 1.0),
        'c_ctx': nrm(8, (D_MODEL,), 1.0),
        'ada_w': nrm(9, (DEPTH, D_MODEL, 6 * D_MODEL), 0.5 * D_MODEL ** -0.5),
        'ada_b': nrm(10, (DEPTH, 6 * D_MODEL), 0.02),
        'norm_mix_g': 1.0 + nrm(11, (DEPTH, D_MODEL), 0.02),
        'norm_ffn_g': 1.0 + nrm(12, (DEPTH, D_MODEL), 0.02),
        'w_in': nrm(13, (DEPTH, D_MODEL, IN_DIM), D_MODEL ** -0.5),
        'q_norm_a': 1.0 + nrm(14, (DEPTH, HEAD_DIM), 0.02),
        'k_norm_a': 1.0 + nrm(15, (DEPTH, HEAD_DIM), 0.02),
        'q_norm_b': 1.0 + nrm(16, (DEPTH, HEAD_DIM), 0.02),
        'k_norm_b': 1.0 + nrm(17, (DEPTH, HEAD_DIM), 0.02),
        'na_rel_bias': nrm(18, (DEPTH, H_A, 2 * NA_ROWS - 1, 2 * NA_COLS - 1), 0.02),
        'ret_decay_logit': base_logit[None, None, :] + nrm(19, (DEPTH, 2, H_C), 0.1),
        'ret_norm_g': 1.0 + nrm(20, (DEPTH, H_C * HEAD_DIM), 0.02),
        'w_out': nrm(21, (DEPTH, MIX_DIM, D_MODEL), MIX_DIM ** -0.5),
        'w_up': nrm(22, (DEPTH, D_MODEL, 2 * FFN_DIM), D_MODEL ** -0.5),
        'conv_w': nrm(23, (DEPTH, CONV_WIDTH, 2 * FFN_DIM), CONV_WIDTH ** -0.5),
        'conv_b': nrm(24, (DEPTH, 2 * FFN_DIM), 0.01),
        'w_down': nrm(25, (DEPTH, FFN_DIM, D_MODEL), FFN_DIM ** -0.5),
    }


def reference(x_prompt, x_sample, cache_ka, cache_va, cache_kb, cache_vb, state_ret, c, c_ctx,
              ada_w, ada_b, norm_mix_g, norm_ffn_g, w_in, q_norm_a, k_norm_a, q_norm_b, k_norm_b,
              na_rel_bias, ret_decay_logit, ret_norm_g, w_out, w_up, conv_w, conv_b, w_down):
    xp = x_prompt
    xs = x_sample
    Bp, S, _ = xp.shape
    Bs, T, _ = xs.shape
    rope_cos, rope_sin = axial_rope_tables(T)
    ka_list, va_list, kb_list, vb_list, st_list = [], [], [], [], []
    for l in range(DEPTH):
        m_ctx = adaln(c_ctx[None, :], ada_w[l], ada_b[l])
        m_lat = adaln(c, ada_w[l], ada_b[l])

        h = modulate(rms_norm(xp, norm_mix_g[l]), m_ctx[0], m_ctx[1])
        qa, ka, va, qb, kb, vb, qc, kc, vc, gc = head_inputs(h, w_in[l], q_norm_a[l], k_norm_a[l],
                                                             q_norm_b[l], k_norm_b[l])
        oa = block_attention(qa[:, :, :, None, :], ka, va).reshape(Bp, S, H_A * HEAD_DIM)
        ob = block_attention(qb.reshape(Bp, S, KV_B, G_B, HEAD_DIM), kb, vb).reshape(Bp, S, H_B * HEAD_DIM)
        zero_state = jnp.zeros((Bp, H_C, HEAD_DIM, HEAD_DIM), f32)
        oc, s_f, s_b = retention_mixer(qc, kc, vc, gc, ret_decay_logit[l], ret_norm_g[l], zero_state, zero_state)
        xp = xp + m_ctx[2] * (jnp.concatenate([oa, ob, oc], axis=-1) @ w_out[l])
        h = modulate(rms_norm(xp, norm_ffn_g[l]), m_ctx[3], m_ctx[4])
        xp = xp + m_ctx[5] * conv_ffn(h, w_up[l], conv_w[l], conv_b[l], w_down[l])
        ka_list.append(ka)
        va_list.append(va)
        kb_list.append(kb)
        vb_list.append(vb)
        st_list.append(jnp.stack([s_f, s_b], axis=1))

        h = modulate(rms_norm(xs, norm_mix_g[l]), m_lat[0], m_lat[1])
        qa, ka, va, qb, kb, vb, qc, kc, vc, gc = head_inputs(h, w_in[l], q_norm_a[l], k_norm_a[l],
                                                             q_norm_b[l], k_norm_b[l])
        oa = neighbourhood_attention(qa, ka, va, cache_ka[:, l], cache_va[:, l],
                                     na_rel_bias[l]).reshape(Bs, T, H_A * HEAD_DIM)
        qb_r = apply_axial_rope(qb, rope_cos, rope_sin)
        kb_r = apply_axial_rope(kb, rope_cos, rope_sin)
        k_all = jnp.concatenate([kb_r, cache_kb[:, l]], axis=1)
        v_all = jnp.concatenate([vb, cache_vb[:, l]], axis=1)
        ob = block_attention(qb_r.reshape(Bs, T, KV_B, G_B, HEAD_DIM), k_all, v_all).reshape(Bs, T, H_B * HEAD_DIM)
        oc, _, _ = retention_mixer(qc, kc, vc, gc, ret_decay_logit[l], ret_norm_g[l],
                                   state_ret[:, l, 0], state_ret[:, l, 1])
        xs = xs + m_lat[2] * (jnp.concatenate([oa, ob, oc], axis=-1) @ w_out[l])
        h = modulate(rms_norm(xs, norm_ffn_g[l]), m_lat[3], m_lat[4])
        xs = xs + m_lat[5] * conv_ffn(h, w_up[l], conv_w[l], conv_b[l], w_down[l])

    new_cache_ka = jnp.stack(ka_list, axis=1)
    new_cache_va = jnp.stack(va_list, axis=1)
    new_cache_kb = jnp.stack(kb_list, axis=1)
    new_cache_vb = jnp.stack(vb_list, axis=1)
    new_state_ret = jnp.stack(st_list, axis=1)
    return (xp, xs, new_cache_ka, new_cache_va, new_cache_kb, new_cache_vb, new_state_ret)
```

```python
import functools

import jax
import jax.numpy as jnp
from jax import lax
from jax.experimental import pallas as pl
from jax.experimental.pallas import tpu as pltpu

D_MODEL = 2048
DEPTH = 2
GRID_W = 64
HEAD_DIM = 128
H_A = 4
H_B = 8
KV_B = 2
G_B = H_B // KV_B
H_C = 4
NA_ROWS = 8
NA_COLS = 16
ROPE_THETA = 10000.0
FFN_DIM = 5504
RMS_EPS = 1e-6
NEG_INF = -1e30
IN_DIM = 3 * H_A * HEAD_DIM + (H_B + 2 * KV_B) * HEAD_DIM + 4 * H_C * HEAD_DIM

QA0, KA0, VA0 = 0, H_A, 2 * H_A
QB0 = 3 * H_A
KB0 = QB0 + H_B
VB0 = KB0 + KV_B
QC0 = VB0 + KV_B
KC0, VC0, GC0 = QC0 + H_C, QC0 + 2 * H_C, QC0 + 3 * H_C

LANE = 128
ROW_TILE = 1024
FFN_CHUNK = 256
FFN_PAD = 5632
Q_CHUNK = 256
ADA_TILE = 1024
MOD_ROWS = 8
MIB = 1024 * 1024

f32 = jnp.float32
bf16 = jnp.bfloat16


def _cparams(sem, vmem_mib):
    return pltpu.CompilerParams(dimension_semantics=sem, vmem_limit_bytes=vmem_mib * MIB)


def _dot(a, b):
    return jnp.dot(a, b, preferred_element_type=f32)


def _dot_nt(a, b):
    return lax.dot_general(a, b, (((1,), (1,)), ((), ())), preferred_element_type=f32)


def _silu(x):
    return x / (1.0 + jnp.exp(-x))


def _rms(x, g):
    return x * lax.rsqrt(jnp.mean(x * x, axis=-1, keepdims=True) + RMS_EPS) * g


def _ada_kernel(cond_ref, w_ref, b_ref, o_ref):
    s = _silu(cond_ref[...]).astype(bf16)
    o_ref[...] = _dot(s, w_ref[...].astype(bf16)) + b_ref[...]


def _ada(cond, ada_w, ada_b):
    n = ada_w.shape[-1]
    return pl.pallas_call(
        _ada_kernel,
        grid=(DEPTH, n // ADA_TILE),
        in_specs=[
            pl.BlockSpec((MOD_ROWS, D_MODEL), lambda l, j: (0, 0)),
            pl.BlockSpec((None, D_MODEL, ADA_TILE), lambda l, j: (l, 0, j)),
            pl.BlockSpec((None, 1, ADA_TILE), lambda l, j: (l, 0, j)),
        ],
        out_specs=pl.BlockSpec((None, MOD_ROWS, ADA_TILE), lambda l, j: (l, 0, j)),
        out_shape=jax.ShapeDtypeStruct((DEPTH, MOD_ROWS, n), f32),
        compiler_params=_cparams(("parallel", "parallel"), 40),
        name="ada",
    )(cond, ada_w, ada_b.reshape(DEPTH, 1, n))


NORM_ROWS = 64


def _fill_normed(x_ref, g_ref, sh_ref, sc_ref, h_ref):
    g = g_ref[...]
    sh = sh_ref[...]
    sc1 = 1.0 + sc_ref[...]

    def body(r, carry):
        rows = pl.ds(pl.multiple_of(r * NORM_ROWS, NORM_ROWS), NORM_ROWS)
        h_ref[rows, :] = (_rms(x_ref[rows, :], g) * sc1 + sh).astype(bf16)
        return carry

    lax.fori_loop(0, x_ref.shape[0] // NORM_ROWS, body, 0)


def _inproj_kernel(x_ref, g_ref, sh_ref, sc_ref, w_ref, o_ref, h_ref):
    @pl.when(pl.program_id(1) == 0)
    def _():
        _fill_normed(x_ref, g_ref, sh_ref, sc_ref, h_ref)

    o_ref[...] = _dot(h_ref[...], w_ref[...]).astype(o_ref.dtype)


def _up_kernel(x_ref, g_ref, sh_ref, sc_ref, w_ref, cw_ref, cb_ref, o_ref, h_ref, *, seq):
    @pl.when(pl.program_id(1) == 0)
    def _():
        _fill_normed(x_ref, g_ref, sh_ref, sc_ref, h_ref)

    tn = w_ref.shape[1]
    cw = cw_ref[...]
    cb = cb_ref[...]
    pos = lax.broadcasted_iota(jnp.int32, (seq, tn), 0)
    first = pos == 0
    last = pos == seq - 1
    for c in range(x_ref.shape[0] // seq):
        rows = pl.ds(c * seq, seq)
        u = _dot(h_ref[rows, :], w_ref[...])
        up = jnp.where(first, 0.0, pltpu.roll(u, 1, 0))
        dn = jnp.where(last, 0.0, pltpu.roll(u, seq - 1, 0))
        u = up * cw[0:1] + u * cw[1:2] + dn * cw[2:3] + cb
        o_ref[rows, :] = (_silu(u[:, : tn // 2]) * u[:, tn // 2:]).astype(o_ref.dtype)


def _mod_spec(which, row_of_tile):
    return pl.BlockSpec((None, None, 1, D_MODEL), lambda i, j: (row_of_tile(i), which, 0, 0))


def _normed_matmul(x, gain, mods, which, row_of_tile, w, tn, *, conv=None, seq=None):
    m = x.shape[0]
    n = w.shape[1]
    in_specs = [
        pl.BlockSpec((ROW_TILE, D_MODEL), lambda i, j: (i, 0)),
        pl.BlockSpec((1, D_MODEL), lambda i, j: (0, 0)),
        _mod_spec(which, row_of_tile),
        _mod_spec(which + 1, row_of_tile),
        pl.BlockSpec((D_MODEL, tn), lambda i, j: (0, j)),
    ]
    args = [x, gain, mods, mods, w]
    if conv is None:
        body, n_out, tn_out = _inproj_kernel, n, tn
    else:
        body = functools.partial(_up_kernel, seq=seq)
        n_out, tn_out = n // 2, tn // 2
        in_specs += [pl.BlockSpec((3, tn), lambda i, j: (0, j)),
                     pl.BlockSpec((1, tn), lambda i, j: (0, j))]
        args += list(conv)
    return pl.pallas_call(
        body,
        grid=(m // ROW_TILE, n // tn),
        in_specs=in_specs,
        out_specs=pl.BlockSpec((ROW_TILE, tn_out), lambda i, j: (i, j)),
        out_shape=jax.ShapeDtypeStruct((m, n_out), bf16),
        scratch_shapes=[pltpu.VMEM((ROW_TILE, D_MODEL), bf16)],
        compiler_params=_cparams(("parallel", "arbitrary"), 48),
        name="normed_matmul" if conv is None else "up_conv_gate",
    )(*args)


def _resid_kernel(*refs, n_in):
    a_refs, w_refs = refs[:n_in], refs[n_in:2 * n_in]
    x_ref, gate_ref, o_ref = refs[2 * n_in:]
    acc = _dot(a_refs[0][...], w_refs[0][...])
    for a_ref, w_ref in zip(a_refs[1:], w_refs[1:]):
        acc += _dot(a_ref[...], w_ref[...])
    o_ref[...] = x_ref[...] + gate_ref[...] * acc


def _resid_matmul(acts, ws, x, mods, which, row_of_tile, tn):
    m, n = x.shape
    in_specs = [pl.BlockSpec((ROW_TILE, a.shape[1]), lambda i, j: (i, 0)) for a in acts]
    in_specs += [pl.BlockSpec((w.shape[0], tn), lambda i, j: (0, j)) for w in ws]
    in_specs += [pl.BlockSpec((ROW_TILE, tn), lambda i, j: (i, j)),
                 pl.BlockSpec((None, None, 1, tn), lambda i, j: (row_of_tile(i), which, 0, j))]
    return pl.pallas_call(
        functools.partial(_resid_kernel, n_in=len(acts)),
        grid=(m // ROW_TILE, n // tn),
        in_specs=in_specs,
        out_specs=pl.BlockSpec((ROW_TILE, tn), lambda i, j: (i, j)),
        out_shape=jax.ShapeDtypeStruct((m, n), f32),
        compiler_params=_cparams(("parallel", "parallel"), 52),
        name="resid_matmul",
    )(*acts, *ws, x, mods)


def _softmax_attend(q, ks, vs, biases, scale):
    scores = []
    for k, b in zip(ks, biases):
        s = _dot_nt(q, k) * scale
        scores.append(s if b is None else s + b)
    mx = scores[0].max(axis=-1, keepdims=True)
    for s in scores[1:]:
        mx = jnp.maximum(mx, s.max(axis=-1, keepdims=True))
    ps = [jnp.exp(s - mx) for s in scores]
    denom = ps[0].sum(axis=-1, keepdims=True)
    for p in ps[1:]:
        denom += p.sum(axis=-1, keepdims=True)
    o = _dot(ps[0].astype(bf16), vs[0])
    for p, v in zip(ps[1:], vs[1:]):
        o += _dot(p.astype(bf16), v)
    return o / denom


def _log_sigmoid_vec(x):
    v = jnp.full((1, 1), x, f32)
    return jnp.minimum(v, 0.0) - jnp.log1p(jnp.exp(-jnp.abs(v)))


def _retention_weights(lg_f, lg_b, row0, n_rows, t):
    i = lax.broadcasted_iota(jnp.int32, (n_rows, t), 0) + row0
    j = lax.broadcasted_iota(jnp.int32, (n_rows, t), 1)
    diff = (i - j).astype(f32)
    arg = jnp.where(diff > 0, lg_f * diff, -lg_b * diff)
    return jnp.where(diff == 0, 2.0, jnp.exp(arg))


def _rope(x, cos, sin_signed):
    lane = lax.broadcasted_iota(jnp.int32, x.shape, 1)
    partner = jnp.where((lane & 63) < 32, pltpu.roll(x, 96, 1), pltpu.roll(x, 32, 1))
    return x * cos + partner * sin_signed


def _prompt_mixer_kernel(dl_ref, p_ref, hg_ref, rg_ref,
                         o_ref, ka_ref, va_ref, kb_ref, vb_ref, st_ref):
    t = p_ref.shape[0]
    scale = HEAD_DIM ** -0.5

    def col(c):
        return p_ref[:, c * LANE:(c + 1) * LANE]

    qg_a, kg_a, qg_b, kg_b = (hg_ref[i:i + 1, :] for i in range(4))

    for h in range(H_A):
        k = _rms(col(KA0 + h).astype(f32), kg_a)
        v = col(VA0 + h)
        ka_ref[:, h * LANE:(h + 1) * LANE] = k
        va_ref[:, h * LANE:(h + 1) * LANE] = v.astype(f32)
        q = _rms(col(QA0 + h).astype(f32), qg_a).astype(bf16)
        o = _softmax_attend(q, [k.astype(bf16)], [v], [None], scale)
        o_ref[:, h * LANE:(h + 1) * LANE] = o.astype(bf16)

    for kv in range(KV_B):
        k = _rms(col(KB0 + kv).astype(f32), kg_b)
        v = col(VB0 + kv)
        kb_ref[:, kv * LANE:(kv + 1) * LANE] = k
        vb_ref[:, kv * LANE:(kv + 1) * LANE] = v.astype(f32)
        kb = k.astype(bf16)
        for g in range(G_B):
            h = kv * G_B + g
            q = _rms(col(QB0 + h).astype(f32), qg_b).astype(bf16)
            o = _softmax_attend(q, [kb], [v], [None], scale)
            o_ref[:, (H_A + h) * LANE:(H_A + h + 1) * LANE] = o.astype(bf16)

    pos = lax.broadcasted_iota(jnp.int32, (t, LANE), 0).astype(f32)
    for h in range(H_C):
        lg_f = _log_sigmoid_vec(dl_ref[h])
        lg_b = _log_sigmoid_vec(dl_ref[H_C + h])
        q, k, v = col(QC0 + h), col(KC0 + h), col(VC0 + h)
        s = _dot_nt(q, k) * scale * _retention_weights(lg_f, lg_b, 0, t, t)
        o = _dot(s.astype(bf16), v)
        o = _rms(o, rg_ref[:, h * LANE:(h + 1) * LANE]) * _silu(col(GC0 + h).astype(f32))
        c0 = (H_A + H_B + h) * LANE
        o_ref[:, c0:c0 + LANE] = o.astype(bf16)
        ks = k.astype(f32) * scale
        kf = (ks * jnp.exp(lg_f * (t - 1.0 - pos))).T.astype(bf16)
        kb_ = (ks * jnp.exp(lg_b * pos)).T.astype(bf16)
        st_ref[0, h] = _dot(kf, v)
        st_ref[1, h] = _dot(kb_, v)


def _prompt_mixer(p, seq, head_gains, ret_gain, decay_logit):
    m = p.shape[0]
    nb = m // seq
    out_shape = (
        jax.ShapeDtypeStruct((m, (H_A + H_B + H_C) * HEAD_DIM), bf16),
        jax.ShapeDtypeStruct((m, H_A * HEAD_DIM), f32),
        jax.ShapeDtypeStruct((m, H_A * HEAD_DIM), f32),
        jax.ShapeDtypeStruct((m, KV_B * HEAD_DIM), f32),
        jax.ShapeDtypeStruct((m, KV_B * HEAD_DIM), f32),
        jax.ShapeDtypeStruct((nb, 2, H_C, HEAD_DIM, HEAD_DIM), f32),
    )
    row_spec = lambda width: pl.BlockSpec((seq, width), lambda b: (b, 0))
    return pl.pallas_call(
        _prompt_mixer_kernel,
        grid=(nb,),
        in_specs=[
            pl.BlockSpec(memory_space=pltpu.SMEM),
            row_spec(IN_DIM),
            pl.BlockSpec((4, HEAD_DIM), lambda b: (0, 0)),
            pl.BlockSpec((1, H_C * HEAD_DIM), lambda b: (0, 0)),
        ],
        out_specs=(
            row_spec((H_A + H_B + H_C) * HEAD_DIM),
            row_spec(H_A * HEAD_DIM), row_spec(H_A * HEAD_DIM),
            row_spec(KV_B * HEAD_DIM), row_spec(KV_B * HEAD_DIM),
            pl.BlockSpec((None, 2, H_C, HEAD_DIM, HEAD_DIM), lambda b: (b, 0, 0, 0, 0)),
        ),
        out_shape=out_shape,
        compiler_params=_cparams(("parallel",), 32),
        name="prompt_mixer",
    )(decay_logit.reshape(2 * H_C), p, head_gains, ret_gain)


def _sample_attn_kernel(*refs, use_bias, use_rope):
    q_ref, k_ref, v_ref, ck_ref, cv_ref, qg_ref, kg_ref = refs[:7]
    rest = list(refs[7:])
    bias_ref = rest.pop(0) if use_bias else None
    cos_ref, sin_ref = (rest.pop(0), rest.pop(0)) if use_rope else (None, None)
    (o_ref,) = rest
    t = q_ref.shape[0]
    scale = HEAD_DIM ** -0.5
    q = _rms(q_ref[...].astype(f32), qg_ref[...])
    k = _rms(k_ref[...].astype(f32), kg_ref[...])
    if use_rope:
        q = _rope(q, cos_ref[...], sin_ref[...])
        k = _rope(k, cos_ref[...], sin_ref[...])
    q = q.astype(bf16)
    ks = [k.astype(bf16), ck_ref[...].astype(bf16)]
    vs = [v_ref[...], cv_ref[...].astype(bf16)]
    for c in range(t // Q_CHUNK):
        rows = slice(c * Q_CHUNK, (c + 1) * Q_CHUNK)
        bias = bias_ref[rows, :] if use_bias else None
        o = _softmax_attend(q[rows], ks, vs, [bias, None], scale)
        o_ref[rows, :] = o.astype(bf16)


def _sample_attn(p, seq, cache_k, cache_v, layer, q0, k0, v0, n_q, group, q_gain, k_gain,
                 bias=None, rope=None):
    m = p.shape[0]
    nb = m // seq
    ctx = cache_k.shape[2]
    head = lambda c0, div: pl.BlockSpec((seq, LANE), lambda h, b: (b, c0 + h // div))
    cache = pl.BlockSpec((None, None, ctx, LANE), lambda h, b: (b, layer, 0, h // group))
    gain = pl.BlockSpec((1, LANE), lambda h, b: (0, 0))
    in_specs = [head(q0, 1), head(k0, group), head(v0, group), cache, cache, gain, gain]
    args = [p, p, p, cache_k, cache_v, q_gain, k_gain]
    if bias is not None:
        in_specs.append(pl.BlockSpec((None, None, seq, seq), lambda h, b: (layer, h, 0, 0)))
        args.append(bias)
    if rope is not None:
        in_specs += [pl.BlockSpec((seq, LANE), lambda h, b: (0, 0))] * 2
        args += list(rope)
    return pl.pallas_call(
        functools.partial(_sample_attn_kernel, use_bias=bias is not None, use_rope=rope is not None),
        grid=(n_q, nb),
        in_specs=in_specs,
        out_specs=pl.BlockSpec((seq, LANE), lambda h, b: (b, h)),
        out_shape=jax.ShapeDtypeStruct((m, n_q * LANE), bf16),
        compiler_params=_cparams(("parallel", "parallel"), 40),
        name="sample_attn",
    )(*args)


def _sample_ret_kernel(dl_ref, q_ref, k_ref, v_ref, g_ref, st_ref, rg_ref, o_ref):
    h = pl.program_id(0)
    t = q_ref.shape[0]
    scale = HEAD_DIM ** -0.5
    lg_f = _log_sigmoid_vec(dl_ref[h])
    lg_b = _log_sigmoid_vec(dl_ref[H_C + h])
    k = k_ref[...]
    v = v_ref[...]
    s_f = st_ref[0].astype(bf16)
    s_b = st_ref[1].astype(bf16)
    for c in range(t // Q_CHUNK):
        rows = slice(c * Q_CHUNK, (c + 1) * Q_CHUNK)
        q = q_ref[rows, :]
        s = _dot_nt(q, k) * scale * _retention_weights(lg_f, lg_b, c * Q_CHUNK, Q_CHUNK, t)
        pos = (lax.broadcasted_iota(jnp.int32, (Q_CHUNK, LANE), 0) + c * Q_CHUNK).astype(f32)
        o = (_dot(s.astype(bf16), v)
             + _dot(q, s_f) * jnp.exp(lg_f * (pos + 1.0))
             + _dot(q, s_b) * jnp.exp(lg_b * (t - pos)))
        o = _rms(o, rg_ref[...]) * _silu(g_ref[rows, :].astype(f32))
        o_ref[rows, :] = o.astype(bf16)


def _sample_ret(p, seq, state, layer, ret_gain, decay_logit):
    m = p.shape[0]
    head = lambda c0: pl.BlockSpec((seq, LANE), lambda h, b: (b, c0 + h))
    return pl.pallas_call(
        _sample_ret_kernel,
        grid=(H_C, m // seq),
        in_specs=[
            pl.BlockSpec(memory_space=pltpu.SMEM),
            head(QC0), head(KC0), head(VC0), head(GC0),
            pl.BlockSpec((None, None, 2, None, HEAD_DIM, HEAD_DIM), lambda h, b: (b, layer, 0, h, 0, 0)),
            pl.BlockSpec((1, LANE), lambda h, b: (0, h)),
        ],
        out_specs=pl.BlockSpec((seq, LANE), lambda h, b: (b, h)),
        out_shape=jax.ShapeDtypeStruct((m, H_C * LANE), bf16),
        compiler_params=_cparams(("parallel", "parallel"), 32),
        name="sample_ret",
    )(decay_logit.reshape(2 * H_C), p, p, p, p, state, ret_gain)


def _na_bias_kernel(rb_ref, o_ref, *, rows):
    base = (pl.program_id(0) * H_A + pl.program_id(1)) * (2 * NA_ROWS - 1) * (2 * NA_COLS - 1)
    kr = min(NA_ROWS, rows)
    qc = lax.broadcasted_iota(jnp.int32, (GRID_W, GRID_W), 0)
    kc = lax.broadcasted_iota(jnp.int32, (GRID_W, GRID_W), 1)
    dc = kc - qc + (NA_COLS - 1)
    c_start = jnp.clip(qc - NA_COLS // 2, 0, GRID_W - NA_COLS)
    in_win = (kc >= c_start) & (kc < c_start + NA_COLS)
    neg = jnp.full((GRID_W, GRID_W), NEG_INF, f32)

    tiles = []
    for dr in range(2 * NA_ROWS - 1):
        tile = neg
        for d in range(2 * NA_COLS - 1):
            tile = jnp.where(dc == d, rb_ref[base + dr * (2 * NA_COLS - 1) + d], tile)
        tiles.append(jnp.where(in_win, tile, neg))

    def tile_for(r, key_row):
        r_start = min(max(r - kr // 2, 0), rows - kr)
        if r_start <= key_row < r_start + kr:
            return tiles[key_row - r + NA_ROWS - 1]
        return neg

    for r in range(rows):
        for pair in range(rows // 2):
            blk = jnp.concatenate([tile_for(r, 2 * pair), tile_for(r, 2 * pair + 1)], axis=1)
            o_ref[r * GRID_W:(r + 1) * GRID_W, pair * 2 * GRID_W:(pair + 1) * 2 * GRID_W] = blk


def _na_bias(rel_bias, seq):
    rows = seq // GRID_W
    return pl.pallas_call(
        functools.partial(_na_bias_kernel, rows=rows),
        grid=(DEPTH, H_A),
        in_specs=[pl.BlockSpec(memory_space=pltpu.SMEM)],
        out_specs=pl.BlockSpec((None, None, seq, seq), lambda l, h: (l, h, 0, 0)),
        out_shape=jax.ShapeDtypeStruct((DEPTH, H_A, seq, seq), f32),
        compiler_params=_cparams(("parallel", "parallel"), 32),
        name="na_bias",
    )(rel_bias.reshape(-1))


def _rope_tables(seq):
    t = jnp.arange(seq)
    nf = HEAD_DIM // 4
    inv_freq = jnp.power(ROPE_THETA, -jnp.arange(nf, dtype=f32) / nf)
    ang_r = (t // GRID_W).astype(f32)[:, None] * inv_freq
    ang_c = (t % GRID_W).astype(f32)[:, None] * inv_freq
    cos = jnp.concatenate([jnp.cos(ang_r)] * 2 + [jnp.cos(ang_c)] * 2, axis=1)
    sin = jnp.concatenate([-jnp.sin(ang_r), jnp.sin(ang_r), -jnp.sin(ang_c), jnp.sin(ang_c)], axis=1)
    return cos, sin


def _interleave_ffn(a):
    lead = a.shape[:-1]
    halves = a.reshape(*lead, 2, FFN_DIM)
    halves = jnp.pad(halves, [(0, 0)] * len(lead) + [(0, 0), (0, FFN_PAD - FFN_DIM)])
    blocks = halves.reshape(*lead, 2, FFN_PAD // FFN_CHUNK, FFN_CHUNK)
    return jnp.swapaxes(blocks, -3, -2).reshape(*lead, 2 * FFN_PAD)


def kernel(x_prompt, x_sample, cache_ka, cache_va, cache_kb, cache_vb, state_ret, c, c_ctx,
           ada_w, ada_b, norm_mix_g, norm_ffn_g, w_in, q_norm_a, k_norm_a, q_norm_b, k_norm_b,
           na_rel_bias, ret_decay_logit, ret_norm_g, w_out, w_up, conv_w, conv_b, w_down):
    bp, s, d = x_prompt.shape
    bs, t, _ = x_sample.shape
    past = cache_ka.shape[2]
    assert d == D_MODEL and ROW_TILE % s == 0 and t == ROW_TILE

    xp = x_prompt.reshape(bp * s, d)
    xs = x_sample.reshape(bs * t, d)
    cka = cache_ka.reshape(bs, DEPTH, past, H_A * HEAD_DIM)
    cva = cache_va.reshape(bs, DEPTH, past, H_A * HEAD_DIM)
    ckb = cache_kb.reshape(bs, DEPTH, past, KV_B * HEAD_DIM)
    cvb = cache_vb.reshape(bs, DEPTH, past, KV_B * HEAD_DIM)

    cond = jnp.zeros((MOD_ROWS, d), f32).at[0].set(c_ctx).at[1:1 + bs].set(c)
    mods = _ada(cond, ada_w, ada_b).reshape(DEPTH, MOD_ROWS, 6, 1, d)
    ctx_row = lambda i: 0
    lat_row = lambda i: 1 + i

    na_bias = _na_bias(na_rel_bias, t)
    rope = _rope_tables(t)

    ka_l, va_l, kb_l, vb_l, st_l = [], [], [], [], []
    for l in range(DEPTH):
        m_l = mods[l]
        w_in_l = w_in[l].astype(bf16)
        w_out_l = w_out[l].astype(bf16)
        w_up_l = _interleave_ffn(w_up[l]).astype(bf16)
        conv_l = (_interleave_ffn(conv_w[l]), _interleave_ffn(conv_b[l])[None, :])
        w_down_l = jnp.pad(w_down[l], ((0, FFN_PAD - FFN_DIM), (0, 0))).astype(bf16)
        g_mix = norm_mix_g[l][None, :]
        g_ffn = norm_ffn_g[l][None, :]
        head_gains = jnp.stack([q_norm_a[l], k_norm_a[l], q_norm_b[l], k_norm_b[l]])
        ret_gain = ret_norm_g[l][None, :]
        wo_a = w_out_l[:H_A * HEAD_DIM]
        wo_b = w_out_l[H_A * HEAD_DIM:(H_A + H_B) * HEAD_DIM]
        wo_c = w_out_l[(H_A + H_B) * HEAD_DIM:]

        p = _normed_matmul(xp, g_mix, m_l, 0, ctx_row, w_in_l, 512)
        o, ka, va, kb, vb, st = _prompt_mixer(p, s, head_gains, ret_gain, ret_decay_logit[l])
        xp = _resid_matmul([o], [w_out_l], xp, m_l, 2, ctx_row, 512)
        act = _normed_matmul(xp, g_ffn, m_l, 3, ctx_row, w_up_l, 2 * FFN_CHUNK, conv=conv_l, seq=s)
        xp = _resid_matmul([act], [w_down_l], xp, m_l, 5, ctx_row, 512)
        ka_l.append(ka.reshape(bp, s, H_A, HEAD_DIM))
        va_l.append(va.reshape(bp, s, H_A, HEAD_DIM))
        kb_l.append(kb.reshape(bp, s, KV_B, HEAD_DIM))
        vb_l.append(vb.reshape(bp, s, KV_B, HEAD_DIM))
        st_l.append(st)

        p = _normed_matmul(xs, g_mix, m_l, 0, lat_row, w_in_l, 512)
        oa = _sample_attn(p, t, cka, cva, l, QA0, KA0, VA0, H_A, 1,
                          head_gains[0:1], head_gains[1:2], bias=na_bias)
        ob = _sample_attn(p, t, ckb, cvb, l, QB0, KB0, VB0, H_B, G_B,
                          head_gains[2:3], head_gains[3:4], rope=rope)
        oc = _sample_ret(p, t, state_ret, l, ret_gain, ret_decay_logit[l])
        xs = _resid_matmul([oa, ob, oc], [wo_a, wo_b, wo_c], xs, m_l, 2, lat_row, 512)
        act = _normed_matmul(xs, g_ffn, m_l, 3, lat_row, w_up_l, 2 * FFN_CHUNK, conv=conv_l, seq=t)
        xs = _resid_matmul([act], [w_down_l], xs, m_l, 5, lat_row, 512)

    return (xp.reshape(bp, s, d), xs.reshape(bs, t, d),
            jnp.stack(ka_l, axis=1), jnp.stack(va_l, axis=1),
            jnp.stack(kb_l, axis=1), jnp.stack(vb_l, axis=1),
            jnp.stack(st_l, axis=1))
```

```python
import functools

import jax
import jax.numpy as jnp
from jax import lax
from jax.experimental import pallas as pl
from jax.experimental.pallas import tpu as pltpu

D_MODEL = 2048
DEPTH = 2
GRID_W = 64
HEAD_DIM = 128
H_A = 4
H_B = 8
KV_B = 2
G_B = H_B // KV_B
H_C = 4
NA_ROWS = 8
NA_COLS = 16
ROPE_THETA = 10000.0
FFN_DIM = 5504
RMS_EPS = 1e-6
NEG_INF = -1e30
IN_DIM = 3 * H_A * HEAD_DIM + (H_B + 2 * KV_B) * HEAD_DIM + 4 * H_C * HEAD_DIM

QA0, KA0, VA0 = 0, H_A, 2 * H_A
QB0 = 3 * H_A
KB0 = QB0 + H_B
VB0 = KB0 + KV_B
QC0 = VB0 + KV_B
KC0, VC0, GC0 = QC0 + H_C, QC0 + 2 * H_C, QC0 + 3 * H_C

LANE = 128
SUBLANES = 8
ROW_TILE = 1024
IN_ROW_TILE = 512
OUT_ROW_TILE = 1024
DOWN_ROW_TILE = 512
UP_TILE = 1024
COL_CHUNK = 1024
RESID_CHUNK = 512
FFN_CHUNK = 256
FFN_PAD = 5632
Q_CHUNK = 256
ADA_TILE = 1024
MOD_ROWS = 8
MIB = 1024 * 1024

f32 = jnp.float32
bf16 = jnp.bfloat16


def _cparams(sem, vmem_mib):
    return pltpu.CompilerParams(dimension_semantics=sem, vmem_limit_bytes=vmem_mib * MIB)


def _dot(a, b):
    return jnp.dot(a, b, preferred_element_type=f32)


def _dot_nt(a, b):
    return lax.dot_general(a, b, (((1,), (1,)), ((), ())), preferred_element_type=f32)


def _silu(x):
    return x / (1.0 + jnp.exp(-x))


def _rms(x, g):
    return x * lax.rsqrt(jnp.mean(x * x, axis=-1, keepdims=True) + RMS_EPS) * g


def _cast_kernel(x_ref, o_ref):
    o_ref[...] = x_ref[...].astype(bf16)


def _interleave_cast_kernel(x_ref, o_ref):
    for p in range(FFN_PAD // FFN_CHUNK):
        valid = min(FFN_CHUNK, FFN_DIM - p * FFN_CHUNK)
        for half in range(2):
            src = half * FFN_DIM + p * FFN_CHUNK
            dst = (2 * p + half) * FFN_CHUNK
            o_ref[:, dst:dst + valid] = x_ref[:, src:src + valid].astype(bf16)
            if valid < FFN_CHUNK:
                o_ref[:, dst + valid:dst + FFN_CHUNK] = jnp.zeros((x_ref.shape[0], FFN_CHUNK - valid), bf16)


def _cast_weights(w, row_block, *, interleave=False):
    depth, r, c = w.shape
    c_out = 2 * FFN_PAD if interleave else c
    return pl.pallas_call(
        _interleave_cast_kernel if interleave else _cast_kernel,
        grid=(depth, r // row_block),
        in_specs=[pl.BlockSpec((None, row_block, c), lambda l, i: (l, i, 0))],
        out_specs=pl.BlockSpec((None, row_block, c_out), lambda l, i: (l, i, 0)),
        out_shape=jax.ShapeDtypeStruct((depth, r, c_out), bf16),
        compiler_params=_cparams(("parallel", "parallel"), 40),
        name="weight_cast",
    )(w)


def _ada_kernel(cond_ref, w_ref, b_ref, o_ref):
    s = _silu(cond_ref[...]).astype(bf16)
    o_ref[...] = _dot(s, w_ref[...].astype(bf16)) + b_ref[...]


def _ada(cond, ada_w, ada_b):
    n = ada_w.shape[-1]
    return pl.pallas_call(
        _ada_kernel,
        grid=(DEPTH, n // ADA_TILE),
        in_specs=[
            pl.BlockSpec((MOD_ROWS, D_MODEL), lambda l, j: (0, 0)),
            pl.BlockSpec((None, D_MODEL, ADA_TILE), lambda l, j: (l, 0, j)),
            pl.BlockSpec((None, 1, ADA_TILE), lambda l, j: (l, 0, j)),
        ],
        out_specs=pl.BlockSpec((None, MOD_ROWS, ADA_TILE), lambda l, j: (l, 0, j)),
        out_shape=jax.ShapeDtypeStruct((DEPTH, MOD_ROWS, n), f32),
        compiler_params=_cparams(("parallel", "parallel"), 40),
        name="ada",
    )(cond, ada_w, ada_b.reshape(DEPTH, 1, n))


NORM_ROWS = 64


def _fill_normed(x_ref, g_ref, sh_ref, sc_ref, h_ref):
    g = g_ref[...]
    sh = sh_ref[...]
    sc1 = 1.0 + sc_ref[...]

    for r in range(x_ref.shape[0] // NORM_ROWS):
        rows = slice(r * NORM_ROWS, (r + 1) * NORM_ROWS)
        h_ref[rows, :] = (_rms(x_ref[rows, :], g) * sc1 + sh).astype(bf16)


def _inproj_kernel(x_ref, g_ref, sh_ref, sc_ref, w_ref, o_ref, h_ref):
    _fill_normed(x_ref, g_ref, sh_ref, sc_ref, h_ref)
    for c0 in range(0, w_ref.shape[1], COL_CHUNK):
        cols = slice(c0, c0 + COL_CHUNK)
        o_ref[:, cols] = _dot(h_ref[...], w_ref[:, cols]).astype(o_ref.dtype)


def _up_kernel(x_ref, g_ref, sh_ref, sc_ref, w_ref, cw_ref, cb_ref, o_ref, h_ref, *, seq):
    @pl.when(pl.program_id(1) == 0)
    def _():
        _fill_normed(x_ref, g_ref, sh_ref, sc_ref, h_ref)

    pair = 2 * FFN_CHUNK
    sub = lax.broadcasted_iota(jnp.int32, (SUBLANES, pair), 0)
    for p in range(w_ref.shape[1] // pair):
        cols = slice(p * pair, (p + 1) * pair)
        cw = cw_ref[:, cols]
        cb = cb_ref[:, cols]
        for c in range(x_ref.shape[0] // seq):
            rows = pl.ds(c * seq, seq)
            u = _dot(h_ref[rows, :], w_ref[:, cols])
            up = pltpu.roll(u, 1, 0)
            dn = pltpu.roll(u, seq - 1, 0)
            up = jnp.concatenate([jnp.where(sub == 0, 0.0, up[:SUBLANES]), up[SUBLANES:]], axis=0)
            dn = jnp.concatenate([dn[:-SUBLANES], jnp.where(sub == SUBLANES - 1, 0.0, dn[-SUBLANES:])], axis=0)
            u = up * cw[0:1] + u * cw[1:2] + dn * cw[2:3] + cb
            act = _silu(u[:, :FFN_CHUNK]) * u[:, FFN_CHUNK:]
            o_ref[rows, p * FFN_CHUNK:(p + 1) * FFN_CHUNK] = act.astype(o_ref.dtype)


def _mod_spec(layer, which, row_of_tile):
    return pl.BlockSpec((None, None, None, 1, D_MODEL),
                        lambda i, *_: (layer, row_of_tile(i), which, 0, 0))


def _resident(block_shape, index_map):
    return pl.BlockSpec(block_shape, index_map, pipeline_mode=pl.Buffered(1))


def _in_projection(x, gain, mods, layer, row_of_tile, w):
    m = x.shape[0]
    n = w.shape[2]
    tm = IN_ROW_TILE
    return pl.pallas_call(
        _inproj_kernel,
        grid=(m // tm,),
        in_specs=[
            pl.BlockSpec((tm, D_MODEL), lambda i: (i, 0)),
            _resident((None, 1, D_MODEL), lambda i: (layer, 0, 0)),
            _mod_spec(layer, 0, functools.partial(row_of_tile, tm)),
            _mod_spec(layer, 1, functools.partial(row_of_tile, tm)),
            _resident((None, D_MODEL, n), lambda i: (layer, 0, 0)),
        ],
        out_specs=pl.BlockSpec((tm, n), lambda i: (i, 0)),
        out_shape=jax.ShapeDtypeStruct((m, n), bf16),
        scratch_shapes=[pltpu.VMEM((tm, D_MODEL), bf16)],
        compiler_params=_cparams(("parallel",), 52),
        name="in_projection",
    )(x, gain, mods, mods, w)


def _up_projection(x, gain, mods, layer, row_of_tile, w, conv, seq):
    m = x.shape[0]
    n = w.shape[2]
    tm, tn = ROW_TILE, UP_TILE
    return pl.pallas_call(
        functools.partial(_up_kernel, seq=seq),
        grid=(m // tm, n // tn),
        in_specs=[
            pl.BlockSpec((tm, D_MODEL), lambda i, j: (i, 0)),
            pl.BlockSpec((None, 1, D_MODEL), lambda i, j: (layer, 0, 0)),
            _mod_spec(layer, 3, functools.partial(row_of_tile, tm)),
            _mod_spec(layer, 4, functools.partial(row_of_tile, tm)),
            pl.BlockSpec((None, D_MODEL, tn), lambda i, j: (layer, 0, j)),
            pl.BlockSpec((None, 3, tn), lambda i, j: (layer, 0, j)),
            pl.BlockSpec((None, 1, tn), lambda i, j: (layer, 0, j)),
        ],
        out_specs=pl.BlockSpec((tm, tn // 2), lambda i, j: (i, j)),
        out_shape=jax.ShapeDtypeStruct((m, FFN_DIM), bf16),
        scratch_shapes=[pltpu.VMEM((tm, D_MODEL), bf16)],
        compiler_params=_cparams(("parallel", "arbitrary"), 52),
        name="up_conv_gate",
    )(x, gain, mods, mods, w, *conv)


def _resid_kernel(*refs, n_in, kw):
    a_refs = refs[:n_in]
    w_ref, x_ref, gate_ref, o_ref = refs[n_in:]
    for c0 in range(0, o_ref.shape[1], RESID_CHUNK):
        cols = slice(c0, c0 + RESID_CHUNK)
        acc = _dot(a_refs[0][...], w_ref[0:kw, cols])
        for p in range(1, n_in):
            acc += _dot(a_refs[p][...], w_ref[p * kw:(p + 1) * kw, cols])
        o_ref[:, cols] = x_ref[:, cols] + gate_ref[:, cols] * acc


def _resid_matmul(parts, kw, w, x, mods, layer, which, row_of_tile, tm):
    m, n = x.shape
    in_specs = [pl.BlockSpec((tm, kw), functools.partial(lambda cb, i: (i, cb), cb)) for _, cb in parts]
    in_specs += [_resident((None, w.shape[1], n), lambda i: (layer, 0, 0)),
                 pl.BlockSpec((tm, n), lambda i: (i, 0)),
                 _mod_spec(layer, which, functools.partial(row_of_tile, tm))]
    return pl.pallas_call(
        functools.partial(_resid_kernel, n_in=len(parts), kw=kw),
        grid=(m // tm,),
        in_specs=in_specs,
        out_specs=pl.BlockSpec((tm, n), lambda i: (i, 0)),
        out_shape=jax.ShapeDtypeStruct((m, n), f32),
        compiler_params=_cparams(("parallel",), 56),
        name="resid_matmul",
    )(*[a for a, _ in parts], w, x, mods)


def _softmax_attend(q, ks, vs, biases, scale):
    scores = []
    for k, b in zip(ks, biases):
        s = _dot_nt(q, k) * scale
        scores.append(s if b is None else s + b)
    mx = scores[0].max(axis=-1, keepdims=True)
    for s in scores[1:]:
        mx = jnp.maximum(mx, s.max(axis=-1, keepdims=True))
    ps = [jnp.exp(s - mx) for s in scores]
    denom = ps[0].sum(axis=-1, keepdims=True)
    for p in ps[1:]:
        denom += p.sum(axis=-1, keepdims=True)
    o = _dot(ps[0].astype(bf16), vs[0])
    for p, v in zip(ps[1:], vs[1:]):
        o += _dot(p.astype(bf16), v)
    return o / denom


def _log_sigmoid_vec(x):
    v = jnp.full((1, 1), x, f32)
    return jnp.minimum(v, 0.0) - jnp.log1p(jnp.exp(-jnp.abs(v)))


def _retention_weights(lg_f, lg_b, row0, n_rows, t):
    i = lax.broadcasted_iota(jnp.int32, (n_rows, t), 0) + row0
    j = lax.broadcasted_iota(jnp.int32, (n_rows, t), 1)
    diff = (i - j).astype(f32)
    arg = jnp.where(diff > 0, lg_f * diff, -lg_b * diff)
    return jnp.where(diff == 0, 2.0, jnp.exp(arg))


def _rope(x, cos, sin_signed):
    lane = lax.broadcasted_iota(jnp.int32, x.shape, 1)
    partner = jnp.where((lane & 63) < 32, pltpu.roll(x, 96, 1), pltpu.roll(x, 32, 1))
    return x * cos + partner * sin_signed


def _prompt_mixer_kernel(*refs, layer):
    dl_ref, p_ref, hg_ref, rg_ref = refs[:4]
    o_ref, ka_ref, va_ref, kb_ref, vb_ref, st_ref = refs[-6:]
    t = p_ref.shape[0]
    scale = HEAD_DIM ** -0.5
    dl0 = layer * 2 * H_C

    def col(c):
        return p_ref[:, c * LANE:(c + 1) * LANE]

    qg_a, kg_a, qg_b, kg_b = (hg_ref[i:i + 1, :] for i in range(4))

    for h in range(H_A):
        k = _rms(col(KA0 + h).astype(f32), kg_a)
        v = col(VA0 + h)
        ka_ref[pl.ds(h, t, stride=H_A), :] = k
        va_ref[pl.ds(h, t, stride=H_A), :] = v.astype(f32)
        q = _rms(col(QA0 + h).astype(f32), qg_a).astype(bf16)
        o = _softmax_attend(q, [k.astype(bf16)], [v], [None], scale)
        o_ref[:, h * LANE:(h + 1) * LANE] = o.astype(bf16)

    for kv in range(KV_B):
        k = _rms(col(KB0 + kv).astype(f32), kg_b)
        v = col(VB0 + kv)
        kb_ref[pl.ds(kv, t, stride=KV_B), :] = k
        vb_ref[pl.ds(kv, t, stride=KV_B), :] = v.astype(f32)
        kb = k.astype(bf16)
        for g in range(G_B):
            h = kv * G_B + g
            q = _rms(col(QB0 + h).astype(f32), qg_b).astype(bf16)
            o = _softmax_attend(q, [kb], [v], [None], scale)
            o_ref[:, (H_A + h) * LANE:(H_A + h + 1) * LANE] = o.astype(bf16)

    pos = lax.broadcasted_iota(jnp.int32, (t, LANE), 0).astype(f32)
    for h in range(H_C):
        lg_f = _log_sigmoid_vec(dl_ref[dl0 + h])
        lg_b = _log_sigmoid_vec(dl_ref[dl0 + H_C + h])
        q, k, v = col(QC0 + h), col(KC0 + h), col(VC0 + h)
        s = _dot_nt(q, k) * scale * _retention_weights(lg_f, lg_b, 0, t, t)
        o = _dot(s.astype(bf16), v)
        o = _rms(o, rg_ref[:, h * LANE:(h + 1) * LANE]) * _silu(col(GC0 + h).astype(f32))
        c0 = (H_A + H_B + h) * LANE
        o_ref[:, c0:c0 + LANE] = o.astype(bf16)
        ks = k.astype(f32) * scale
        kf = (ks * jnp.exp(lg_f * (t - 1.0 - pos))).T.astype(bf16)
        kb_ = (ks * jnp.exp(lg_b * pos)).T.astype(bf16)
        st_ref[0, h] = _dot(kf, v)
        st_ref[1, h] = _dot(kb_, v)


def _prompt_mixer(p, seq, head_gains, ret_gain, decay_logit, layer, caches):
    m = p.shape[0]
    nb = m // seq
    cache_shape = lambda heads: jax.ShapeDtypeStruct((nb, DEPTH, seq * heads, HEAD_DIM), f32)
    cache_spec = lambda heads: pl.BlockSpec((None, None, seq * heads, HEAD_DIM), lambda b: (b, layer, 0, 0))
    out_shape = (
        jax.ShapeDtypeStruct((m, (H_A + H_B + H_C) * HEAD_DIM), bf16),
        cache_shape(H_A), cache_shape(H_A), cache_shape(KV_B), cache_shape(KV_B),
        jax.ShapeDtypeStruct((nb, DEPTH, 2, H_C, HEAD_DIM, HEAD_DIM), f32),
    )
    row_spec = lambda width: pl.BlockSpec((seq, width), lambda b: (b, 0))
    n_fixed = 4
    return pl.pallas_call(
        functools.partial(_prompt_mixer_kernel, layer=layer),
        grid=(nb,),
        in_specs=[
            pl.BlockSpec(memory_space=pltpu.SMEM),
            row_spec(IN_DIM),
            pl.BlockSpec((None, 4, HEAD_DIM), lambda b: (layer, 0, 0)),
            pl.BlockSpec((None, 1, H_C * HEAD_DIM), lambda b: (layer, 0, 0)),
        ] + [pl.BlockSpec(memory_space=pl.ANY)] * len(caches),
        out_specs=(
            row_spec((H_A + H_B + H_C) * HEAD_DIM),
            cache_spec(H_A), cache_spec(H_A), cache_spec(KV_B), cache_spec(KV_B),
            pl.BlockSpec((None, None, 2, H_C, HEAD_DIM, HEAD_DIM), lambda b: (b, layer, 0, 0, 0, 0)),
        ),
        out_shape=out_shape,
        input_output_aliases={n_fixed + i: 1 + i for i in range(len(caches))},
        compiler_params=_cparams(("parallel",), 32),
        name="prompt_mixer",
    )(decay_logit.reshape(-1), p, head_gains, ret_gain, *caches)


def _sample_attn_kernel(*refs, use_bias, use_rope):
    q_ref, k_ref, v_ref, ck_ref, cv_ref, qg_ref, kg_ref = refs[:7]
    rest = list(refs[7:])
    bias_ref = rest.pop(0) if use_bias else None
    cos_ref, sin_ref = (rest.pop(0), rest.pop(0)) if use_rope else (None, None)
    (o_ref,) = rest
    t = q_ref.shape[0]
    scale = HEAD_DIM ** -0.5
    q = _rms(q_ref[...].astype(f32), qg_ref[...])
    k = _rms(k_ref[...].astype(f32), kg_ref[...])
    if use_rope:
        q = _rope(q, cos_ref[...], sin_ref[...])
        k = _rope(k, cos_ref[...], sin_ref[...])
    q = q.astype(bf16)
    ks = [k.astype(bf16), ck_ref[...].astype(bf16)]
    vs = [v_ref[...], cv_ref[...].astype(bf16)]
    for c in range(t // Q_CHUNK):
        rows = slice(c * Q_CHUNK, (c + 1) * Q_CHUNK)
        bias = bias_ref[rows, :] if use_bias else None
        o = _softmax_attend(q[rows], ks, vs, [bias, None], scale)
        o_ref[rows, :] = o.astype(bf16)


def _sample_attn(p, seq, cache_k, cache_v, layer, q0, k0, v0, n_q, group, head_gains, q_gain, k_gain,
                 bias=None, rope=None):
    m = p.shape[0]
    nb = m // seq
    ctx = cache_k.shape[2]
    head = lambda c0, div: pl.BlockSpec((seq, LANE), lambda h, b: (b, c0 + h // div))
    cache = pl.BlockSpec((None, None, ctx, LANE), lambda h, b: (b, layer, 0, h // group))
    gain = lambda row: pl.BlockSpec((None, None, 1, LANE), lambda h, b: (layer, row, 0, 0))
    in_specs = [head(q0, 1), head(k0, group), head(v0, group), cache, cache, gain(q_gain), gain(k_gain)]
    gains4 = head_gains.reshape(DEPTH, 4, 1, HEAD_DIM)
    args = [p, p, p, cache_k, cache_v, gains4, gains4]
    if bias is not None:
        in_specs.append(pl.BlockSpec((None, None, seq, seq), lambda h, b: (layer, h, 0, 0)))
        args.append(bias)
    if rope is not None:
        in_specs += [pl.BlockSpec((seq, LANE), lambda h, b: (0, 0))] * 2
        args += list(rope)
    return pl.pallas_call(
        functools.partial(_sample_attn_kernel, use_bias=bias is not None, use_rope=rope is not None),
        grid=(n_q, nb),
        in_specs=in_specs,
        out_specs=pl.BlockSpec((seq, LANE), lambda h, b: (b, h)),
        out_shape=jax.ShapeDtypeStruct((m, n_q * LANE), bf16),
        compiler_params=_cparams(("parallel", "parallel"), 40),
        name="sample_attn",
    )(*args)


def _sample_ret_kernel(dl_ref, q_ref, k_ref, v_ref, g_ref, st_ref, rg_ref, o_ref, *, layer):
    h = pl.program_id(0)
    t = q_ref.shape[0]
    scale = HEAD_DIM ** -0.5
    lg_f = _log_sigmoid_vec(dl_ref[layer * 2 * H_C + h])
    lg_b = _log_sigmoid_vec(dl_ref[layer * 2 * H_C + H_C + h])
    k = k_ref[...]
    v = v_ref[...]
    s_f = st_ref[0].astype(bf16)
    s_b = st_ref[1].astype(bf16)
    for c in range(t // Q_CHUNK):
        rows = slice(c * Q_CHUNK, (c + 1) * Q_CHUNK)
        q = q_ref[rows, :]
        s = _dot_nt(q, k) * scale * _retention_weights(lg_f, lg_b, c * Q_CHUNK, Q_CHUNK, t)
        pos = (lax.broadcasted_iota(jnp.int32, (Q_CHUNK, LANE), 0) + c * Q_CHUNK).astype(f32)
        o = (_dot(s.astype(bf16), v)
             + _dot(q, s_f) * jnp.exp(lg_f * (pos + 1.0))
             + _dot(q, s_b) * jnp.exp(lg_b * (t - pos)))
        o = _rms(o, rg_ref[...]) * _silu(g_ref[rows, :].astype(f32))
        o_ref[rows, :] = o.astype(bf16)


def _sample_ret(p, seq, state, layer, ret_gain, decay_logit):
    m = p.shape[0]
    head = lambda c0: pl.BlockSpec((seq, LANE), lambda h, b: (b, c0 + h))
    return pl.pallas_call(
        functools.partial(_sample_ret_kernel, layer=layer),
        grid=(H_C, m // seq),
        in_specs=[
            pl.BlockSpec(memory_space=pltpu.SMEM),
            head(QC0), head(KC0), head(VC0), head(GC0),
            pl.BlockSpec((None, None, 2, None, HEAD_DIM, HEAD_DIM), lambda h, b: (b, layer, 0, h, 0, 0)),
            pl.BlockSpec((None, 1, LANE), lambda h, b: (layer, 0, h)),
        ],
        out_specs=pl.BlockSpec((seq, LANE), lambda h, b: (b, h)),
        out_shape=jax.ShapeDtypeStruct((m, H_C * LANE), bf16),
        compiler_params=_cparams(("parallel", "parallel"), 32),
        name="sample_ret",
    )(decay_logit.reshape(-1), p, p, p, p, state, ret_gain)


def _na_bias_kernel(rb_ref, o_ref, *, rows):
    base = (pl.program_id(0) * H_A + pl.program_id(1)) * (2 * NA_ROWS - 1) * (2 * NA_COLS - 1)
    kr = min(NA_ROWS, rows)
    qc = lax.broadcasted_iota(jnp.int32, (GRID_W, GRID_W), 0)
    kc = lax.broadcasted_iota(jnp.int32, (GRID_W, GRID_W), 1)
    dc = kc - qc + (NA_COLS - 1)
    c_start = jnp.clip(qc - NA_COLS // 2, 0, GRID_W - NA_COLS)
    in_win = (kc >= c_start) & (kc < c_start + NA_COLS)
    neg = jnp.full((GRID_W, GRID_W), NEG_INF, f32)

    tiles = []
    for dr in range(2 * NA_ROWS - 1):
        tile = neg
        for d in range(2 * NA_COLS - 1):
            tile = jnp.where(dc == d, rb_ref[base + dr * (2 * NA_COLS - 1) + d], tile)
        tiles.append(jnp.where(in_win, tile, neg))

    def tile_for(r, key_row):
        r_start = min(max(r - kr // 2, 0), rows - kr)
        if r_start <= key_row < r_start + kr:
            return tiles[key_row - r + NA_ROWS - 1]
        return neg

    for r in range(rows):
        for pair in range(rows // 2):
            blk = jnp.concatenate([tile_for(r, 2 * pair), tile_for(r, 2 * pair + 1)], axis=1)
            o_ref[r * GRID_W:(r + 1) * GRID_W, pair * 2 * GRID_W:(pair + 1) * 2 * GRID_W] = blk


def _na_bias(rel_bias, seq):
    rows = seq // GRID_W
    return pl.pallas_call(
        functools.partial(_na_bias_kernel, rows=rows),
        grid=(DEPTH, H_A),
        in_specs=[pl.BlockSpec(memory_space=pltpu.SMEM)],
        out_specs=pl.BlockSpec((None, None, seq, seq), lambda l, h: (l, h, 0, 0)),
        out_shape=jax.ShapeDtypeStruct((DEPTH, H_A, seq, seq), f32),
        compiler_params=_cparams(("parallel", "parallel"), 32),
        name="na_bias",
    )(rel_bias.reshape(-1))


def _rope_tables(seq):
    t = jnp.arange(seq)
    nf = HEAD_DIM // 4
    inv_freq = jnp.power(ROPE_THETA, -jnp.arange(nf, dtype=f32) / nf)
    ang_r = (t // GRID_W).astype(f32)[:, None] * inv_freq
    ang_c = (t % GRID_W).astype(f32)[:, None] * inv_freq
    cos = jnp.concatenate([jnp.cos(ang_r)] * 2 + [jnp.cos(ang_c)] * 2, axis=1)
    sin = jnp.concatenate([-jnp.sin(ang_r), jnp.sin(ang_r), -jnp.sin(ang_c), jnp.sin(ang_c)], axis=1)
    return cos, sin


def _interleave_ffn(a):
    lead = a.shape[:-1]
    halves = a.reshape(*lead, 2, FFN_DIM)
    halves = jnp.pad(halves, [(0, 0)] * len(lead) + [(0, 0), (0, FFN_PAD - FFN_DIM)])
    blocks = halves.reshape(*lead, 2, FFN_PAD // FFN_CHUNK, FFN_CHUNK)
    return jnp.swapaxes(blocks, -3, -2).reshape(*lead, 2 * FFN_PAD)


def kernel(x_prompt, x_sample, cache_ka, cache_va, cache_kb, cache_vb, state_ret, c, c_ctx,
           ada_w, ada_b, norm_mix_g, norm_ffn_g, w_in, q_norm_a, k_norm_a, q_norm_b, k_norm_b,
           na_rel_bias, ret_decay_logit, ret_norm_g, w_out, w_up, conv_w, conv_b, w_down):
    bp, s, d = x_prompt.shape
    bs, t, _ = x_sample.shape
    past = cache_ka.shape[2]
    assert d == D_MODEL and ROW_TILE % s == 0 and t == ROW_TILE

    xp = x_prompt.reshape(bp * s, d)
    xs = x_sample.reshape(bs * t, d)
    cka = cache_ka.reshape(bs, DEPTH, past, H_A * HEAD_DIM)
    cva = cache_va.reshape(bs, DEPTH, past, H_A * HEAD_DIM)
    ckb = cache_kb.reshape(bs, DEPTH, past, KV_B * HEAD_DIM)
    cvb = cache_vb.reshape(bs, DEPTH, past, KV_B * HEAD_DIM)

    cond = jnp.zeros((MOD_ROWS, d), f32).at[0].set(c_ctx).at[1:1 + bs].set(c)
    mods = _ada(cond, ada_w, ada_b).reshape(DEPTH, MOD_ROWS, 6, 1, d)
    ctx_row = lambda tm, i: 0
    lat_row = lambda tm, i: 1 + (i * tm) // t

    na_bias = _na_bias(na_rel_bias, t)
    rope = _rope_tables(t)

    w_in_b = _cast_weights(w_in, 256)
    w_out_b = _cast_weights(w_out, 512)
    w_up_b = _cast_weights(w_up, 128, interleave=True)
    w_down_b = _cast_weights(w_down, FFN_DIM // 8)
    conv = (_interleave_ffn(conv_w), _interleave_ffn(conv_b)[:, None, :])
    g_mix = norm_mix_g[:, None, :]
    g_ffn = norm_ffn_g[:, None, :]
    head_gains = jnp.stack([q_norm_a, k_norm_a, q_norm_b, k_norm_b], axis=1)
    ret_gain = ret_norm_g[:, None, :]
    kw_mix = H_A * HEAD_DIM

    caches = ()
    for l in range(DEPTH):
        p = _in_projection(xp, g_mix, mods, l, ctx_row, w_in_b)
        o, *caches = _prompt_mixer(p, s, head_gains, ret_gain, ret_decay_logit, l, caches)
        xp = _resid_matmul([(o, 0)], o.shape[1], w_out_b, xp, mods, l, 2, ctx_row, OUT_ROW_TILE)
        act = _up_projection(xp, g_ffn, mods, l, ctx_row, w_up_b, conv, s)
        xp = _resid_matmul([(act, 0)], FFN_DIM, w_down_b, xp, mods, l, 5, ctx_row, DOWN_ROW_TILE)

        p = _in_projection(xs, g_mix, mods, l, lat_row, w_in_b)
        oa = _sample_attn(p, t, cka, cva, l, QA0, KA0, VA0, H_A, 1, head_gains, 0, 1, bias=na_bias)
        ob = _sample_attn(p, t, ckb, cvb, l, QB0, KB0, VB0, H_B, G_B, head_gains, 2, 3, rope=rope)
        oc = _sample_ret(p, t, state_ret, l, ret_gain, ret_decay_logit)
        xs = _resid_matmul([(oa, 0), (ob, 0), (ob, 1), (oc, 0)], kw_mix, w_out_b,
                           xs, mods, l, 2, lat_row, OUT_ROW_TILE)
        act = _up_projection(xs, g_ffn, mods, l, lat_row, w_up_b, conv, t)
        xs = _resid_matmul([(act, 0)], FFN_DIM, w_down_b, xs, mods, l, 5, lat_row, DOWN_ROW_TILE)

    ka, va, kb, vb, st = caches
    return (xp.reshape(bp, s, d), xs.reshape(bs, t, d),
            ka.reshape(bp, DEPTH, s, H_A, HEAD_DIM), va.reshape(bp, DEPTH, s, H_A, HEAD_DIM),
            kb.reshape(bp, DEPTH, s, KV_B, HEAD_DIM), vb.reshape(bp, DEPTH, s, KV_B, HEAD_DIM),
            st)
```
